```python
import math
import jax, jax.numpy as jnp
from jax import lax
import numpy as np

D_MODEL = 1024
BATCH = 8
SEQ = 2048
DEPTH = 2

CONV_WIDTH = D_MODEL
CONV_K = 3
HEAD_DIM = 64
V_HEAD_DIM = 2 * HEAD_DIM
N_DIFF_HEADS = D_MODEL // V_HEAD_DIM
QK_WIDTH = N_DIFF_HEADS * 2 * HEAD_DIM
ATTN_WIDTH = N_DIFF_HEADS * V_HEAD_DIM
ROT_DIM = HEAD_DIM // 4
ROPE_THETA = 500000.0
Q_BLOCK = 128
D_FF = ((8 * D_MODEL // 3 + 255) // 256) * 256
NORM_EPS = 1e-6
N_BRANCHES = 2
IN_WIDTH = 3 * CONV_WIDTH + 2 * QK_WIDTH + ATTN_WIDTH + N_BRANCHES * D_MODEL

kernel_name = "hybrid_gated_conv_diffattn_macaron_encoder"


def rmsnorm(x, w):
    xf = x.astype(jnp.float32)
    y = xf * lax.rsqrt(jnp.mean(xf * xf, axis=-1, keepdims=True) + NORM_EPS)
    return (y * w.astype(jnp.float32)).astype(x.dtype)


def swiglu(x, w13, w2):
    g, u = jnp.split(x @ w13, 2, axis=-1)
    return (jax.nn.silu(g) * u) @ w2


def rope_tables(seq_len):
    pos = jnp.arange(seq_len, dtype=jnp.float32)
    inv = ROPE_THETA ** (-jnp.arange(0, ROT_DIM, 2, dtype=jnp.float32) / ROT_DIM)
    ang = pos[:, None] * inv[None, :]
    return jnp.cos(ang), jnp.sin(ang)


def apply_partial_rope(t, cos, sin):
    c = cos[None, :, None, None, :].astype(t.dtype)
    s = sin[None, :, None, None, :].astype(t.dtype)
    half = ROT_DIM // 2
    t1 = t[..., :half]
    t2 = t[..., half:ROT_DIM]
    return jnp.concatenate([t1 * c - t2 * s, t2 * c + t1 * s, t[..., ROT_DIM:]], axis=-1)


def short_conv(u, w):
    up = jnp.pad(u, ((0, 0), (1, 1), (0, 0)))
    s = u.shape[1]
    return up[:, 0:s] * w[0] + up[:, 1:s + 1] * w[1] + up[:, 2:s + 2] * w[2]


def diff_attention(q, k, v, lam, lam_init, subln_w):
    b, s = q.shape[0], q.shape[1]
    nb = s // Q_BLOCK
    qb = q.reshape(b, nb, Q_BLOCK, N_DIFF_HEADS, 2, HEAD_DIM).transpose(1, 0, 3, 4, 2, 5)
    kt = k.transpose(0, 2, 3, 1, 4)
    vt = v.transpose(0, 2, 1, 3)
    scale = HEAD_DIM ** -0.5

    def block(qblk):
        sc = jnp.einsum('bhcqd,bhckd->bhcqk', qblk, kt).astype(jnp.float32) * scale
        p = jax.nn.softmax(sc, axis=-1)
        a = p[:, :, 0] - lam * p[:, :, 1]
        return jnp.einsum('bhqk,bhkd->bhqd', a.astype(vt.dtype), vt)

    o = lax.map(block, qb)
    o = o.transpose(1, 0, 3, 2, 4).reshape(b, s, N_DIFF_HEADS, V_HEAD_DIM)
    o = rmsnorm(o, subln_w) * (1.0 - lam_init)
    return o.reshape(b, s, ATTN_WIDTH)


def token_mixer(h, w_in, b_gate, conv_w, w_conv_out, w_attn_out, lam_vec, subln_w, w_o,
                lam_init, cos, sin):
    b, s = h.shape[0], h.shape[1]
    z = h @ w_in
    c0 = CONV_WIDTH
    q0 = 3 * c0
    idx = [c0, 2 * c0, q0, q0 + QK_WIDTH, q0 + 2 * QK_WIDTH, q0 + 2 * QK_WIDTH + ATTN_WIDTH]
    bg, cg, hc, q, k, v, gl = jnp.split(z, idx, axis=-1)
    y_conv = (bg * short_conv(cg * hc, conv_w)) @ w_conv_out
    q = apply_partial_rope(q.reshape(b, s, N_DIFF_HEADS, 2, HEAD_DIM), cos, sin)
    k = apply_partial_rope(k.reshape(b, s, N_DIFF_HEADS, 2, HEAD_DIM), cos, sin)
    v = v.reshape(b, s, N_DIFF_HEADS, V_HEAD_DIM)
    lv = lam_vec.astype(jnp.float32)
    lam = jnp.exp(jnp.sum(lv[0] * lv[1])) - jnp.exp(jnp.sum(lv[2] * lv[3])) + lam_init
    y_attn = diff_attention(q, k, v, lam, lam_init, subln_w) @ w_attn_out
    g_conv, g_attn = jnp.split(jax.nn.sigmoid(gl + b_gate), 2, axis=-1)
    return (g_conv * y_conv + g_attn * y_attn) @ w_o


def setup_inputs(seed: int = 0) -> dict:
    key = jax.random.key(seed)
    ks = jax.random.split(key, 14)
    nrm = jax.random.normal
    D = D_MODEL
    return {
        "x": nrm(ks[0], (BATCH, SEQ, D), jnp.float32),
        "w_in": nrm(ks[1], (DEPTH, D, IN_WIDTH), jnp.float32) * D ** -0.5,
        "b_gate": nrm(ks[2], (DEPTH, N_BRANCHES * D), jnp.float32) * 0.01,
        "conv_w": nrm(ks[3], (DEPTH, CONV_K, CONV_WIDTH), jnp.float32) * CONV_K ** -0.5,
        "w_conv_out": nrm(ks[4], (DEPTH, CONV_WIDTH, D), jnp.float32) * CONV_WIDTH ** -0.5,
        "w_attn_out": nrm(ks[5], (DEPTH, ATTN_WIDTH, D), jnp.float32) * ATTN_WIDTH ** -0.5,
        "lam_vec": nrm(ks[6], (DEPTH, 4, HEAD_DIM), jnp.float32) * 0.1,
        "subln_w": 1.0 + 0.02 * nrm(ks[7], (DEPTH, V_HEAD_DIM), jnp.float32),
        "w_o": nrm(ks[8], (DEPTH, D, D), jnp.float32) * D ** -0.5,
        "ffn1_w13": nrm(ks[9], (DEPTH, D, 2 * D_FF), jnp.float32) * D ** -0.5,
        "ffn1_w2": nrm(ks[10], (DEPTH, D_FF, D), jnp.float32) * D_FF ** -0.5,
        "ffn2_w13": nrm(ks[11], (DEPTH, D, 2 * D_FF), jnp.float32) * D ** -0.5,
        "ffn2_w2": nrm(ks[12], (DEPTH, D_FF, D), jnp.float32) * D_FF ** -0.5,
        "norm_w": 1.0 + 0.02 * nrm(ks[13], (DEPTH, 6, D), jnp.float32),
    }


def reference(x, w_in, b_gate, conv_w, w_conv_out, w_attn_out, lam_vec, subln_w, w_o,
              ffn1_w13, ffn1_w2, ffn2_w13, ffn2_w2, norm_w):
    cos, sin = rope_tables(x.shape[1])
    for l in range(DEPTH):
        nw = norm_w[l]
        lam_init = 0.8 - 0.6 * math.exp(-0.3 * l)
        x = x + 0.5 * rmsnorm(swiglu(rmsnorm(x, nw[0]), ffn1_w13[l], ffn1_w2[l]), nw[1])
        m = token_mixer(rmsnorm(x, nw[2]), w_in[l], b_gate[l], conv_w[l], w_conv_out[l],
                        w_attn_out[l], lam_vec[l], subln_w[l], w_o[l], lam_init, cos, sin)
        x = x + rmsnorm(m, nw[3])
        x = x + 0.5 * rmsnorm(swiglu(rmsnorm(x, nw[4]), ffn2_w13[l], ffn2_w2[l]), nw[5])
    return x
```

```python
import functools
import math

import jax
import jax.numpy as jnp
from jax.experimental import pallas as pl
from jax.experimental.pallas import tpu as pltpu

D_MODEL = 1024
HEAD_DIM = 64
V_HEAD_DIM = 2 * HEAD_DIM
N_HEADS = D_MODEL // V_HEAD_DIM
ROT_DIM = HEAD_DIM // 4
ROPE_THETA = 500000.0
D_FF = 2816
NORM_EPS = 1e-6
DEPTH = 2

MXU_N = 256
LANES = 128
VMEM_LIMIT = 56 * 1024 * 1024

FFN_TM = 512
ATTN_TQ = 256

BF16 = jnp.bfloat16
F32 = jnp.float32


def _dot(a, b):
    return jnp.dot(a, b, preferred_element_type=F32)


def _rmsnorm(x, w):
    return x * jax.lax.rsqrt(jnp.mean(x * x, axis=-1, keepdims=True) + NORM_EPS) * w


def _resident(shape):
    return pl.BlockSpec(shape, lambda *_: (0,) * len(shape), pipeline_mode=pl.Buffered(1))


def _ffn_kernel(x_ref, nw_ref, w13_ref, w2_ref, o_ref, h_ref, *, pre, post):
    x = x_ref[...]
    xn = _rmsnorm(x, nw_ref[pre:pre + 1, :]).astype(BF16)
    for c in range(D_FF // MXU_N):
        lo = c * MXU_N
        g = _dot(xn, w13_ref[:, lo:lo + MXU_N])
        u = _dot(xn, w13_ref[:, D_FF + lo:D_FF + lo + MXU_N])
        h_ref[:, lo:lo + MXU_N] = (g * jax.nn.sigmoid(g) * u).astype(BF16)
    y = _dot(h_ref[...], w2_ref[...])
    o_ref[...] = x + 0.5 * _rmsnorm(y, nw_ref[post:post + 1, :])


def _ffn(x2d, nw, w13, w2, pre, post):
    t = x2d.shape[0]
    return pl.pallas_call(
        functools.partial(_ffn_kernel, pre=pre, post=post),
        grid=(t // FFN_TM,),
        in_specs=[
            pl.BlockSpec((FFN_TM, D_MODEL), lambda i: (i, 0)),
            _resident(nw.shape),
            _resident(w13.shape),
            _resident(w2.shape),
        ],
        out_specs=pl.BlockSpec((FFN_TM, D_MODEL), lambda i: (i, 0)),
        out_shape=jax.ShapeDtypeStruct(x2d.shape, F32),
        scratch_shapes=[pltpu.VMEM((FFN_TM, D_FF), BF16)],
        compiler_params=pltpu.CompilerParams(
            dimension_semantics=("parallel",), vmem_limit_bytes=VMEM_LIMIT),
        name="ffn",
    )(x2d, nw, w13, w2)


def _norm_once(x_ref, nw_ref, xn_ref):
    @pl.when(pl.program_id(1) == 0)
    def _():
        xn_ref[...] = _rmsnorm(x_ref[0], nw_ref[2:3, :]).astype(BF16)


def _conv_proj_kernel(x_ref, nw_ref, wb_ref, wc_ref, wh_ref, cw_ref, o_ref, xn_ref):
    _norm_once(x_ref, nw_ref, xn_ref)
    xn = xn_ref[...]
    s = xn.shape[0]
    u = _dot(xn, wc_ref[...]) * _dot(xn, wh_ref[...])
    row = jax.lax.broadcasted_iota(jnp.int32, (s, 1), 0)
    u_prev = jnp.where(row == 0, 0.0, pltpu.roll(u, 1, 0))
    u_next = jnp.where(row == s - 1, 0.0, pltpu.roll(u, s - 1, 0))
    conv = u_prev * cw_ref[0:1, :] + u * cw_ref[1:2, :] + u_next * cw_ref[2:3, :]
    o_ref[0] = (_dot(xn, wb_ref[...]) * conv).astype(BF16)


def _qk_proj_kernel(x_ref, nw_ref, w_ref, c_ref, sa_ref, sb_ref, o_ref, xn_ref, *, n_q_blocks):
    _norm_once(x_ref, nw_ref, xn_ref)
    t = _dot(xn_ref[...], w_ref[...])
    width = t.shape[1]
    half = ROT_DIM // 2
    t = t * c_ref[...] + pltpu.roll(t, width - half, 1) * sa_ref[...] + pltpu.roll(t, half, 1) * sb_ref[...]
    scale = jnp.where(pl.program_id(1) < n_q_blocks, HEAD_DIM ** -0.5, 1.0)
    o_ref[0] = (t * scale).astype(BF16)


def _vg_proj_kernel(x_ref, nw_ref, w_ref, b_ref, o_ref, xn_ref, *, n_v_blocks):
    _norm_once(x_ref, nw_ref, xn_ref)
    z = _dot(xn_ref[...], w_ref[...])
    j = pl.program_id(1)

    @pl.when(j < n_v_blocks)
    def _():
        o_ref[0] = z.astype(BF16)

    @pl.when(j >= n_v_blocks)
    def _():
        o_ref[0] = jax.nn.sigmoid(z + b_ref[...]).astype(BF16)


def _proj_call(kernel, x, nw, extra_inputs, extra_specs, n_col_blocks, name):
    b, s, d = x.shape
    return pl.pallas_call(
        kernel,
        grid=(b, n_col_blocks),
        in_specs=[pl.BlockSpec((1, s, d), lambda i, j: (i, 0, 0)), _resident(nw.shape)] + extra_specs,
        out_specs=pl.BlockSpec((1, s, MXU_N), lambda i, j: (i, 0, j)),
        out_shape=jax.ShapeDtypeStruct((b, s, n_col_blocks * MXU_N), BF16),
        scratch_shapes=[pltpu.VMEM((s, d), BF16)],
        compiler_params=pltpu.CompilerParams(
            dimension_semantics=("parallel", "arbitrary"), vmem_limit_bytes=VMEM_LIMIT),
        name=name,
    )(x, nw, *extra_inputs)


def _w_cols(first_block):
    return pl.BlockSpec((D_MODEL, MXU_N), lambda i, j: (0, first_block + j))


def _attn_kernel(q_ref, k_ref, v_ref, lv_ref, sw_ref, o_ref, *, lam_init):
    q = q_ref[0]
    k = k_ref[0]
    lane = jax.lax.broadcasted_iota(jnp.int32, (1, V_HEAD_DIM), 1)
    nt = (((1,), (1,)), ((), ()))
    s1 = jax.lax.dot_general(jnp.where(lane < HEAD_DIM, q, 0), k, nt, preferred_element_type=F32)
    s2 = jax.lax.dot_general(jnp.where(lane >= HEAD_DIM, q, 0), k, nt, preferred_element_type=F32)
    p1 = jnp.exp(s1 - jnp.max(s1, axis=-1, keepdims=True))
    p2 = jnp.exp(s2 - jnp.max(s2, axis=-1, keepdims=True))
    lv = lv_ref[...]
    lam = (jnp.exp(jnp.sum(lv[0:1] * lv[1:2], axis=-1, keepdims=True))
           - jnp.exp(jnp.sum(lv[2:3] * lv[3:4], axis=-1, keepdims=True)) + lam_init)
    r1 = 1.0 / jnp.sum(p1, axis=-1, keepdims=True)
    r2 = lam / jnp.sum(p2, axis=-1, keepdims=True)
    a = (p1 * r1 - p2 * r2).astype(BF16)
    o = _dot(a, v_ref[0])
    o_ref[0] = (_rmsnorm(o, sw_ref[...]) * (1.0 - lam_init)).astype(BF16)


def _attention(qk, vg, lam_vec, subln_w, lam_init):
    b, s, _ = qk.shape
    return pl.pallas_call(
        functools.partial(_attn_kernel, lam_init=lam_init),
        grid=(b, N_HEADS, s // ATTN_TQ),
        in_specs=[
            pl.BlockSpec((1, ATTN_TQ, V_HEAD_DIM), lambda i, h, t: (i, t, h)),
            pl.BlockSpec((1, s, V_HEAD_DIM), lambda i, h, t: (i, 0, N_HEADS + h)),
            pl.BlockSpec((1, s, V_HEAD_DIM), lambda i, h, t: (i, 0, h)),
            _resident(lam_vec.shape),
            _resident(subln_w.shape),
        ],
        out_specs=pl.BlockSpec((1, ATTN_TQ, V_HEAD_DIM), lambda i, h, t: (i, t, h)),
        out_shape=jax.ShapeDtypeStruct((b, s, D_MODEL), BF16),
        compiler_params=pltpu.CompilerParams(
            dimension_semantics=("parallel", "parallel", "arbitrary"), vmem_limit_bytes=VMEM_LIMIT),
        name="diff_attention",
    )(qk, qk, vg, lam_vec, subln_w)


def _merge_kernel(x_ref, nw_ref, ac_ref, ao_ref, gc_ref, ga_ref, wc_ref, wa_ref, wo_ref, o_ref):
    y_conv = _dot(ac_ref[...], wc_ref[...])
    y_attn = _dot(ao_ref[...], wa_ref[...])
    m = (gc_ref[...].astype(F32) * y_conv + ga_ref[...].astype(F32) * y_attn).astype(BF16)
    o_ref[...] = x_ref[...] + _rmsnorm(_dot(m, wo_ref[...]), nw_ref[3:4, :])


def _merge(x2d, nw, a_conv, a_attn, vg, w_conv_out, w_attn_out, w_o):
    t = x2d.shape[0]
    rows = lambda col_block: pl.BlockSpec((FFN_TM, D_MODEL), lambda i: (i, col_block))
    return pl.pallas_call(
        _merge_kernel,
        grid=(t // FFN_TM,),
        in_specs=[rows(0), _resident(nw.shape), rows(0), rows(0), rows(1), rows(2),
                  _resident(w_conv_out.shape), _resident(w_attn_out.shape), _resident(w_o.shape)],
        out_specs=rows(0),
        out_shape=jax.ShapeDtypeStruct(x2d.shape, F32),
        compiler_params=pltpu.CompilerParams(
            dimension_semantics=("parallel",), vmem_limit_bytes=VMEM_LIMIT),
        name="merge_out",
    )(x2d, nw, a_conv, a_attn, vg, vg, w_conv_out, w_attn_out, w_o)


def _rope_tables(seq_len, width):
    half = ROT_DIM // 2
    pos = jnp.arange(seq_len, dtype=F32)
    inv = ROPE_THETA ** (-jnp.arange(0, ROT_DIM, 2, dtype=F32) / ROT_DIM)
    ang = pos[:, None] * inv[None, :]
    cos, sin = jnp.cos(ang), jnp.sin(ang)
    zeros = jnp.zeros((seq_len, HEAD_DIM - ROT_DIM), F32)
    zh = jnp.zeros((seq_len, half), F32)
    c = jnp.concatenate([cos, cos, zeros + 1.0], axis=-1)
    sa = jnp.concatenate([-sin, zh, zeros], axis=-1)
    sb = jnp.concatenate([zh, sin, zeros], axis=-1)
    reps = width // HEAD_DIM
    return tuple(jnp.tile(a, (1, reps)) for a in (c, sa, sb))


def kernel(x, w_in, b_gate, conv_w, w_conv_out, w_attn_out, lam_vec, subln_w, w_o,
           ffn1_w13, ffn1_w2, ffn2_w13, ffn2_w2, norm_w):
    b, s, d = x.shape
    assert d == D_MODEL and s % ATTN_TQ == 0 and (b * s) % FFN_TM == 0
    rope = _rope_tables(s, MXU_N)
    rope_specs = [_resident(r.shape) for r in rope]
    conv_blocks = D_MODEL // MXU_N
    qk_blocks = 2 * D_MODEL // MXU_N
    v_blocks = D_MODEL // MXU_N
    vg_blocks = 3 * D_MODEL // MXU_N

    for l in range(DEPTH):
        lam_init = 0.8 - 0.6 * math.exp(-0.3 * l)
        nw = norm_w[l]
        w_in_l = w_in[l].astype(BF16)
        x = _ffn(x.reshape(b * s, d), nw, ffn1_w13[l].astype(BF16), ffn1_w2[l].astype(BF16), 0, 1)
        x = x.reshape(b, s, d)

        a_conv = _proj_call(
            _conv_proj_kernel, x, nw, [w_in_l, w_in_l, w_in_l, conv_w[l]],
            [_w_cols(0), _w_cols(conv_blocks), _w_cols(2 * conv_blocks),
             pl.BlockSpec((3, MXU_N), lambda i, j: (0, j))],
            conv_blocks, "conv_proj")
        qk = _proj_call(
            functools.partial(_qk_proj_kernel, n_q_blocks=qk_blocks // 2), x, nw, [w_in_l, *rope],
            [_w_cols(3 * conv_blocks)] + rope_specs, qk_blocks, "qk_proj")
        vg = _proj_call(
            functools.partial(_vg_proj_kernel, n_v_blocks=v_blocks), x, nw,
            [w_in_l, b_gate[l].reshape(1, -1)],
            [_w_cols(3 * conv_blocks + qk_blocks),
             pl.BlockSpec((1, MXU_N), lambda i, j: (0, jnp.maximum(j - v_blocks, 0)))],
            vg_blocks, "vg_proj")
        a_attn = _attention(qk, vg, lam_vec[l], subln_w[l].reshape(1, -1), lam_init)

        x = _merge(x.reshape(b * s, d), nw, a_conv.reshape(b * s, d), a_attn.reshape(b * s, d),
                   vg.reshape(b * s, 3 * d), w_conv_out[l].astype(BF16), w_attn_out[l].astype(BF16),
                   w_o[l].astype(BF16))
        x = _ffn(x, nw, ffn2_w13[l].astype(BF16), ffn2_w2[l].astype(BF16), 4, 5).reshape(b, s, d)
    return x
```
